```python
import math, functools
import jax, jax.numpy as jnp
from jax import lax
import numpy as np

D_MODEL = 1024
BATCH = 4
SEQ = 4096
DEPTH = 1
DEC_BATCH = 32
DEC_SEQ = 1
PAST_LEN = 16384
PAGE_SIZE = 128

DA_HEADS = 8
DA_HEAD_DIM = 64
DA_V_DIM = 2 * DA_HEAD_DIM
DA_Q_BLOCK = 128
GLA_HEADS = 4
GLA_DK = D_MODEL // 2 // GLA_HEADS
GLA_DV = D_MODEL // GLA_HEADS
GLA_GATE_RANK = 16
GLA_GATE_NORMALIZER = 16.0
GLA_CHUNK = 32
PEER_HEADS = 8
N_KEYS = 128
N_EXPERTS = N_KEYS * N_KEYS
PEER_KEY_DIM = 256
PEER_HALF = PEER_KEY_DIM // 2
PEER_TOPK = 16
PEER_TOKEN_BLOCK = 128
RMS_EPS = 1e-6

DA_Q_COLS = DA_HEADS * 2 * DA_HEAD_DIM
DA_K_COLS = DA_HEADS * 2 * DA_HEAD_DIM
DA_V_COLS = DA_HEADS * DA_V_DIM
GLA_Q_COLS = GLA_HEADS * GLA_DK
GLA_K_COLS = GLA_HEADS * GLA_DK
GLA_V_COLS = GLA_HEADS * GLA_DV
GLA_G_COLS = GLA_GATE_RANK
GLA_R_COLS = GLA_HEADS * GLA_DV
GATE_COLS = 2 * D_MODEL
IN_COL_SIZES = (DA_Q_COLS, DA_K_COLS, DA_V_COLS, GLA_Q_COLS, GLA_K_COLS, GLA_V_COLS, GLA_G_COLS, GLA_R_COLS, GATE_COLS)
W_IN_COLS = DA_Q_COLS + DA_K_COLS + DA_V_COLS + GLA_Q_COLS + GLA_K_COLS + GLA_V_COLS + GLA_G_COLS + GLA_R_COLS + GATE_COLS

kernel_name = 'diffattn_gla_peer_hybrid_step'


def _rms(x, w):
    xf = x.astype(jnp.float32)
    y = xf * lax.rsqrt(jnp.mean(xf * xf, axis=-1, keepdims=True) + RMS_EPS)
    return (y * w.astype(jnp.float32)).astype(x.dtype)


def _split_cols(z):
    parts = []
    start = 0
    for size in IN_COL_SIZES:
        parts.append(z[..., start:start + size])
        start += size
    return parts


def _online_softmax_step(carry, s, v):
    m, l, acc = carry
    m_new = jnp.maximum(m, jnp.max(s, axis=-1))
    corr = jnp.exp(m - m_new)
    p = jnp.exp(s - m_new[..., None])
    acc = acc * corr[..., None] + jnp.einsum('bhmqk,bkhe->bhmqe', p, v)
    return (m_new, l * corr + jnp.sum(p, axis=-1), acc)


def _diff_attn_prompt(q, k, v):
    B, S, H, _, HD = q.shape
    DV = v.shape[-1]
    qb_len = min(DA_Q_BLOCK, S)
    n_blocks = S // qb_len
    q_blocks = q.reshape(B, n_blocks, qb_len, H, 2, HD).transpose(1, 0, 2, 3, 4, 5)
    kf = k.astype(jnp.float32)
    vf = v.astype(jnp.float32)
    key_pos = jnp.arange(S)

    def block(args):
        qi, start = args
        s = jnp.einsum('bqhmd,bkhmd->bhmqk', qi.astype(jnp.float32), kf)
        q_pos = start + jnp.arange(qb_len)
        s = jnp.where(key_pos[None, :] <= q_pos[:, None], s, -jnp.inf)
        p = jax.nn.softmax(s, axis=-1)
        return jnp.einsum('bhmqk,bkhe->bqhme', p, vf)

    o = lax.map(block, (q_blocks, jnp.arange(n_blocks) * qb_len))
    return o.transpose(1, 0, 2, 3, 4, 5).reshape(B, S, H, 2, DV)


def _diff_attn_sample(q, k, v, cache_k, cache_v, layer, page_table):
    DB, Sd, H, _, HD = q.shape
    DV = v.shape[-1]
    qf = q.astype(jnp.float32)
    carry = (jnp.full((DB, H, 2, Sd), -jnp.inf, jnp.float32),
             jnp.zeros((DB, H, 2, Sd), jnp.float32),
             jnp.zeros((DB, H, 2, Sd, DV), jnp.float32))

    def page_step(c, pages):
        pk = cache_k[layer, pages].astype(jnp.float32)
        pk = pk.reshape(DB, pk.shape[1], H, 2, HD)
        pv = cache_v[layer, pages].astype(jnp.float32)
        s = jnp.einsum('bqhmd,bkhmd->bhmqk', qf, pk)
        return _online_softmax_step(c, s, pv), None

    carry, _ = lax.scan(page_step, carry, page_table.T)
    s = jnp.einsum('bqhmd,bkhmd->bhmqk', qf, k.astype(jnp.float32))
    causal = jnp.tril(jnp.ones((Sd, Sd), dtype=bool))
    s = jnp.where(causal, s, -jnp.inf)
    m, l, acc = _online_softmax_step(carry, s, v.astype(jnp.float32))
    o = acc / l[..., None]
    return o.transpose(0, 3, 1, 2, 4)


def _gla_chunked(q, k, v, log_a, s0):
    B, S, H, DK = q.shape
    DV = v.shape[-1]
    C = min(GLA_CHUNK, S)
    n = -(-S // C)
    pad = n * C - S

    def prep(t):
        t = jnp.pad(t.astype(jnp.float32), ((0, 0), (0, pad), (0, 0), (0, 0)))
        return t.reshape(B, n, C, H, t.shape[-1]).transpose(1, 0, 3, 2, 4)

    qc, kc, vc, gc = prep(q), prep(k), prep(v), prep(log_a)
    b = jnp.cumsum(gc, axis=-2)
    b_last = b[..., -1:, :]
    q_dec = qc * jnp.exp(b)
    k_inv = kc * jnp.exp(-b)
    k_end = kc * jnp.exp(b_last - b)
    decay_last = jnp.exp(b_last[..., 0, :])
    causal = jnp.tril(jnp.ones((C, C), dtype=bool))
    attn = jnp.where(causal, jnp.einsum('nbhtd,nbhsd->nbhts', q_dec, k_inv), 0.0)
    o_intra = jnp.einsum('nbhts,nbhse->nbhte', attn, vc)

    def step(s, xs):
        qd, ke, vv, dl = xs
        o = jnp.einsum('bhtd,bhde->bhte', qd, s)
        s = dl[..., :, None] * s + jnp.einsum('bhsd,bhse->bhde', ke, vv)
        return s, o

    s_fin, o_inter = lax.scan(step, s0.astype(jnp.float32), (q_dec, k_end, vc, decay_last))
    o = (o_intra + o_inter).transpose(1, 0, 3, 2, 4).reshape(B, n * C, H, DV)[:, :S]
    return o.astype(q.dtype), s_fin.astype(s0.dtype)


def _peer(h, w_q_peer, sub_keys1, sub_keys2, expert_u, expert_v):
    shape = h.shape
    xt = h.reshape(-1, D_MODEL)
    T = xt.shape[0]
    tb = min(PEER_TOKEN_BLOCK, T)
    n = -(-T // tb)
    xt = jnp.pad(xt, ((0, n * tb - T), (0, 0))).reshape(n, tb, D_MODEL)
    k1 = sub_keys1.astype(jnp.float32)
    k2 = sub_keys2.astype(jnp.float32)

    def block(xb):
        q = (xb @ w_q_peer).astype(jnp.float32).reshape(tb, PEER_HEADS, 2, PEER_HALF)
        s1 = jnp.einsum('thd,kd->thk', q[:, :, 0], k1)
        s2 = jnp.einsum('thd,kd->thk', q[:, :, 1], k2)
        v1, i1 = lax.top_k(s1, PEER_TOPK)
        v2, i2 = lax.top_k(s2, PEER_TOPK)
        cand = (v1[..., :, None] + v2[..., None, :]).reshape(tb, PEER_HEADS, PEER_TOPK * PEER_TOPK)
        vs, ic = lax.top_k(cand, PEER_TOPK)
        e = (jnp.take_along_axis(i1, ic // PEER_TOPK, axis=-1) * N_KEYS
             + jnp.take_along_axis(i2, ic % PEER_TOPK, axis=-1))
        g = jax.nn.softmax(vs, axis=-1)
        u = expert_u[e]
        act = jax.nn.gelu(jnp.einsum('thkd,td->thk', u, xb).astype(jnp.float32), approximate=False)
        w = (g * act).astype(xb.dtype)
        return jnp.einsum('thk,thkd->td', w, expert_v[e])

    y = lax.map(block, xt).reshape(n * tb, D_MODEL)[:T]
    return y.reshape(shape)


def _layer(x, s0, attend, norm1_w, w_in, q_norm_w, k_norm_w, lam, lam_init, subln_w, w_gla_g2, b_gla_g,
           gla_norm_w, w_br_a, w_br_b, w_out, norm2_w, w_q_peer, sub_keys1, sub_keys2, expert_u, expert_v):
    B, S, _ = x.shape
    h = _rms(x, norm1_w)
    z = h @ w_in
    qa, ka, va, qb, kb, vb, g_lr, r, gates = _split_cols(z)
    qa = _rms(qa.reshape(B, S, DA_HEADS, 2, DA_HEAD_DIM), q_norm_w) * (DA_HEAD_DIM ** -0.5)
    ka = _rms(ka.reshape(B, S, DA_HEADS, 2, DA_HEAD_DIM), k_norm_w)
    va = va.reshape(B, S, DA_HEADS, DA_V_DIM)
    o_a = attend(qa, ka, va)
    diff = o_a[..., 0, :] - lam * o_a[..., 1, :]
    y_a = (_rms(diff, subln_w) * (1.0 - lam_init)).astype(x.dtype).reshape(B, S, DA_HEADS * DA_V_DIM)
    qb = qb.reshape(B, S, GLA_HEADS, GLA_DK) * (GLA_DK ** -0.5)
    kb = kb.reshape(B, S, GLA_HEADS, GLA_DK)
    vb = vb.reshape(B, S, GLA_HEADS, GLA_DV)
    log_a = jax.nn.log_sigmoid((g_lr @ w_gla_g2 + b_gla_g).astype(jnp.float32)) / GLA_GATE_NORMALIZER
    log_a = log_a.reshape(B, S, GLA_HEADS, GLA_DK)
    o_b, s_fin = _gla_chunked(qb, kb, vb, log_a, s0)
    y_b = _rms(o_b, gla_norm_w).reshape(B, S, GLA_HEADS * GLA_DV) * jax.nn.silu(r)
    gate_a, gate_b = jnp.split(gates, 2, axis=-1)
    mixed = jax.nn.sigmoid(gate_a) * (y_a @ w_br_a) + jax.nn.sigmoid(gate_b) * (y_b @ w_br_b)
    x = x + mixed @ w_out
    x = x + _peer(_rms(x, norm2_w), w_q_peer, sub_keys1, sub_keys2, expert_u, expert_v)
    return x, ka.reshape(B, S, DA_HEADS, 2 * DA_HEAD_DIM), va, s_fin


def setup_inputs(seed: int = 0) -> dict:
    key = jax.random.key(seed)
    ks = jax.random.split(key, 32)
    f32 = jnp.float32
    n_pages = PAST_LEN // PAGE_SIZE
    n_phys = (DEC_BATCH * n_pages * 5) // 4

    def nrm(k, shape, scale):
        return jax.random.normal(k, shape, f32) * scale

    perm = jax.random.permutation(ks[5], n_phys)
    page_table = perm[:DEC_BATCH * n_pages].reshape(DEC_BATCH, n_pages).astype(jnp.int32)
    return {
        'x_prompt': nrm(ks[0], (BATCH, SEQ, D_MODEL), 1.0),
        'x_sample': nrm(ks[1], (DEC_BATCH, DEC_SEQ, D_MODEL), 1.0),
        'cache_k': nrm(ks[2], (DEPTH, n_phys, PAGE_SIZE, DA_HEADS, 2 * DA_HEAD_DIM), 1.0),
        'cache_v': nrm(ks[3], (DEPTH, n_phys, PAGE_SIZE, DA_HEADS, DA_V_DIM), 1.0),
        'state_gla': nrm(ks[4], (DEPTH, DEC_BATCH, GLA_HEADS, GLA_DK, GLA_DV), 0.5),
        'page_table': page_table,
        'norm1_w': 1.0 + nrm(ks[6], (DEPTH, D_MODEL), 0.02),
        'w_in': nrm(ks[7], (DEPTH, D_MODEL, W_IN_COLS), D_MODEL ** -0.5),
        'q_norm_w': 1.0 + nrm(ks[8], (DEPTH, DA_HEAD_DIM), 0.02),
        'k_norm_w': 1.0 + nrm(ks[9], (DEPTH, DA_HEAD_DIM), 0.02),
        'lambda_q1': nrm(ks[10], (DEPTH, DA_HEAD_DIM), 0.1),
        'lambda_k1': nrm(ks[11], (DEPTH, DA_HEAD_DIM), 0.1),
        'lambda_q2': nrm(ks[12], (DEPTH, DA_HEAD_DIM), 0.1),
        'lambda_k2': nrm(ks[13], (DEPTH, DA_HEAD_DIM), 0.1),
        'subln_w': 1.0 + nrm(ks[14], (DEPTH, DA_V_DIM), 0.02),
        'w_gla_g2': nrm(ks[15], (DEPTH, GLA_GATE_RANK, GLA_HEADS * GLA_DK), GLA_GATE_RANK ** -0.5),
        'b_gla_g': nrm(ks[16], (DEPTH, GLA_HEADS * GLA_DK), 0.1),
        'gla_norm_w': 1.0 + nrm(ks[17], (DEPTH, GLA_DV), 0.02),
        'w_br_a': nrm(ks[18], (DEPTH, DA_HEADS * DA_V_DIM, D_MODEL), (DA_HEADS * DA_V_DIM) ** -0.5),
        'w_br_b': nrm(ks[19], (DEPTH, GLA_HEADS * GLA_DV, D_MODEL), (GLA_HEADS * GLA_DV) ** -0.5),
        'w_out': nrm(ks[20], (DEPTH, D_MODEL, D_MODEL), D_MODEL ** -0.5),
        'norm2_w': 1.0 + nrm(ks[21], (DEPTH, D_MODEL), 0.02),
        'w_q_peer': nrm(ks[22], (DEPTH, D_MODEL, PEER_HEADS * PEER_KEY_DIM), D_MODEL ** -0.5),
        'sub_keys1': nrm(ks[23], (DEPTH, N_KEYS, PEER_HALF), PEER_HALF ** -0.5),
        'sub_keys2': nrm(ks[24], (DEPTH, N_KEYS, PEER_HALF), PEER_HALF ** -0.5),
        'expert_u': nrm(ks[25], (DEPTH, N_EXPERTS, D_MODEL), D_MODEL ** -0.5),
        'expert_v': nrm(ks[26], (DEPTH, N_EXPERTS, D_MODEL), PEER_HEADS ** -0.5),
    }


def reference(x_prompt, x_sample, cache_k, cache_v, state_gla, page_table, norm1_w, w_in, q_norm_w, k_norm_w,
              lambda_q1, lambda_k1, lambda_q2, lambda_k2, subln_w, w_gla_g2, b_gla_g, gla_norm_w, w_br_a, w_br_b,
              w_out, norm2_w, w_q_peer, sub_keys1, sub_keys2, expert_u, expert_v):
    x_p = x_prompt
    x_s = x_sample
    k_p_list, v_p_list, s_p_list, k_s_list, v_s_list, s_s_list = [], [], [], [], [], []
    for l in range(DEPTH):
        lam_init = 0.8 - 0.6 * math.exp(-0.3 * l)
        lam = (jnp.exp(jnp.sum(lambda_q1[l].astype(jnp.float32) * lambda_k1[l].astype(jnp.float32)))
               - jnp.exp(jnp.sum(lambda_q2[l].astype(jnp.float32) * lambda_k2[l].astype(jnp.float32)))
               + lam_init)
        params = dict(norm1_w=norm1_w[l], w_in=w_in[l], q_norm_w=q_norm_w[l], k_norm_w=k_norm_w[l], lam=lam,
                      lam_init=lam_init, subln_w=subln_w[l], w_gla_g2=w_gla_g2[l], b_gla_g=b_gla_g[l],
                      gla_norm_w=gla_norm_w[l], w_br_a=w_br_a[l], w_br_b=w_br_b[l], w_out=w_out[l],
                      norm2_w=norm2_w[l], w_q_peer=w_q_peer[l], sub_keys1=sub_keys1[l], sub_keys2=sub_keys2[l],
                      expert_u=expert_u[l], expert_v=expert_v[l])
        s0_p = jnp.zeros((x_p.shape[0], GLA_HEADS, GLA_DK, GLA_DV), x_p.dtype)
        x_p, k_p, v_p, s_p = _layer(x_p, s0_p, _diff_attn_prompt, **params)
        attend_s = functools.partial(_diff_attn_sample, cache_k=cache_k, cache_v=cache_v, layer=l,
                                     page_table=page_table)
        x_s, k_s, v_s, s_s = _layer(x_s, state_gla[l], attend_s, **params)
        k_p_list.append(k_p)
        v_p_list.append(v_p)
        s_p_list.append(s_p)
        k_s_list.append(k_s)
        v_s_list.append(v_s)
        s_s_list.append(s_s)
    k_prompt = jnp.stack(k_p_list)
    v_prompt = jnp.stack(v_p_list)
    gla_state_prompt = jnp.stack(s_p_list)
    k_sample = jnp.stack(k_s_list)
    v_sample = jnp.stack(v_s_list)
    gla_state_sample = jnp.stack(s_s_list)
    return (x_p, x_s, k_prompt, v_prompt, gla_state_prompt, k_sample, v_sample, gla_state_sample)
```

```python
import functools
import math

import jax
import jax.numpy as jnp
from jax import lax
from jax.experimental import pallas as pl
from jax.experimental.pallas import tpu as pltpu

f32 = jnp.float32
bf16 = jnp.bfloat16
i32 = jnp.int32

D_MODEL = 1024
PAGE_SIZE = 128
DA_HEADS = 8
DA_HEAD_DIM = 64
DA_V_DIM = 128
GLA_HEADS = 4
GLA_DK = 128
GLA_DV = 256
GLA_GATE_RANK = 16
GLA_GATE_NORMALIZER = 16.0
GLA_CHUNK = 32
PEER_HEADS = 8
N_KEYS = 128
PEER_HALF = 128
PEER_TOPK = 16
RMS_EPS = 1e-6
LAM_INIT = 0.8 - 0.6 * math.exp(-0.3 * 0)

V7X_VMEM_LIMIT_BYTES = 56 * 1024 * 1024
LANES = 128

_NT = (((1,), (1,)), ((), ()))
_TN = (((0,), (0,)), ((), ()))


def _dot(a, b):
    return jnp.dot(a, b, preferred_element_type=f32)


def _dot_nt(a, b):
    return lax.dot_general(a, b, _NT, preferred_element_type=f32)


def _dot_tn(a, b):
    return lax.dot_general(a, b, _TN, preferred_element_type=f32)


def _split2(x):
    hi = x.astype(bf16)
    lo = (x - hi.astype(f32)).astype(bf16)
    return hi, lo


def _split3(x):
    hi = x.astype(bf16)
    r1 = x - hi.astype(f32)
    mid = r1.astype(bf16)
    lo = (r1 - mid.astype(f32)).astype(bf16)
    return hi, mid, lo


def _rms_rows(x, w):
    ms = jnp.mean(x * x, axis=-1, keepdims=True)
    return x * lax.rsqrt(ms + RMS_EPS) * w


def _params(*sem):
    return pltpu.CompilerParams(dimension_semantics=sem, vmem_limit_bytes=V7X_VMEM_LIMIT_BYTES)


def _resident(shape):
    nd = len(shape)
    return pl.BlockSpec(shape, lambda *_: (0,) * nd, pipeline_mode=pl.Buffered(1))


def _inproj_kernel(x_ref, n1_ref, wda_ref, wgla_ref, wg_ref, wr_ref, wgt_ref, qn_ref, kn_ref,
                   g2h_ref, g2l_ref, bg_ref, gsum_ref,
                   qa_ref, ka_ref, va_ref, qb_ref, kb_ref, vb_ref, la_ref, r_ref, gt_ref):
    h = _rms_rows(x_ref[...], n1_ref[...]).astype(bf16)
    gsum = gsum_ref[...]

    def head_norm(z, w_row):
        hi, lo = _split2(z * z)
        ss = _dot(hi, gsum) + _dot(lo, gsum)
        return z * lax.rsqrt(ss * (1.0 / DA_HEAD_DIM) + RMS_EPS) * w_row

    cw = 256
    for c in range(4):
        sl = slice(c * cw, (c + 1) * cw)
        z = _dot(h, wda_ref[:, sl])
        qa_ref[:, sl] = (head_norm(z, qn_ref[...]) * (DA_HEAD_DIM ** -0.5)).astype(bf16)
    for c in range(4):
        sl = slice(c * cw, (c + 1) * cw)
        z = _dot(h, wda_ref[:, 1024 + c * cw:1024 + (c + 1) * cw])
        ka_ref[:, sl] = head_norm(z, kn_ref[...])
    for c in range(4):
        sl = slice(c * cw, (c + 1) * cw)
        va_ref[:, sl] = _dot(h, wda_ref[:, 2048 + c * cw:2048 + (c + 1) * cw])
    for c in range(2):
        sl = slice(c * cw, (c + 1) * cw)
        qb_ref[:, sl] = _dot(h, wgla_ref[:, sl]) * (GLA_DK ** -0.5)
        kb_ref[:, sl] = _dot(h, wgla_ref[:, 512 + c * cw:512 + (c + 1) * cw])
    for c in range(4):
        sl = slice(c * cw, (c + 1) * cw)
        vb_ref[:, sl] = _dot(h, wgla_ref[:, 1024 + c * cw:1024 + (c + 1) * cw])
        r_ref[:, sl] = _dot(h, wr_ref[:, sl])
    for c in range(8):
        sl = slice(c * cw, (c + 1) * cw)
        gt_ref[:, sl] = _dot(h, wgt_ref[:, sl])
    glr_hi, glr_lo = _split2(_dot(h, wg_ref[...]))
    pre = (_dot(glr_hi, g2h_ref[...]) + _dot(glr_lo, g2h_ref[...]) + _dot(glr_hi, g2l_ref[...])
           + bg_ref[...])
    la_ref[...] = jax.nn.log_sigmoid(pre) * (1.0 / GLA_GATE_NORMALIZER)


def _inproj(x2, prm):
    T = x2.shape[0]
    tm = min(T, 256)
    row = lambda w: pl.BlockSpec((tm, w), lambda i: (i, 0))
    ins = [x2, prm["norm1_w"], prm["w_da"], prm["w_gla"], prm["w_g"], prm["w_r"], prm["w_gate"],
           prm["qn"], prm["kn"], prm["g2h"], prm["g2l"], prm["bg"], prm["gsum"]]
    in_specs = [row(D_MODEL)] + [_resident(a.shape) for a in ins[1:]]
    outs = [(1024, bf16), (1024, f32), (1024, f32), (512, f32), (512, f32), (1024, f32), (512, f32),
            (1024, f32), (2048, f32)]
    return pl.pallas_call(
        _inproj_kernel,
        grid=(T // tm,),
        in_specs=in_specs,
        out_specs=[row(w) for w, _ in outs],
        out_shape=[jax.ShapeDtypeStruct((T, w), dt) for w, dt in outs],
        compiler_params=_params("parallel"),
        name="inproj",
    )(*ins)


def _lambda_scalar(lq1, lk1, lq2, lk2):
    return (jnp.exp(jnp.sum(lq1 * lk1, keepdims=True)) - jnp.exp(jnp.sum(lq2 * lk2, keepdims=True))
            + LAM_INIT)


def _subln(o1, o2, lam, subln_w):
    diff = o1 - lam * o2
    return _rms_rows(diff, subln_w) * (1.0 - LAM_INIT)


def _attn_prompt_kernel(q_ref, k_ref, v_ref, lq1_ref, lk1_ref, lq2_ref, lk2_ref, sw_ref, o_ref,
                        kbf_ref, vbf_ref, m_ref, l_ref, acc_ref, *, tq):
    qi = pl.program_id(2)

    @pl.when(qi == 0)
    def _():
        kbf_ref[...] = k_ref[0].astype(bf16)
        vbf_ref[...] = v_ref[0].astype(bf16)

    q = q_ref[0].astype(f32)
    lane = lax.broadcasted_iota(i32, q.shape, 1)
    qs = jnp.concatenate([jnp.where(lane < DA_HEAD_DIM, q, 0.0), jnp.where(lane >= DA_HEAD_DIM, q, 0.0)],
                         axis=0).astype(bf16)
    m_ref[...] = jnp.full(m_ref.shape, -jnp.inf, f32)
    l_ref[...] = jnp.zeros(l_ref.shape, f32)
    acc_ref[...] = jnp.zeros(acc_ref.shape, f32)

    def step(j, masked):
        off = pl.multiple_of(j * tq, tq)
        s = _dot_nt(qs, kbf_ref[pl.ds(off, tq), :])
        if masked:
            r = lax.broadcasted_iota(i32, (tq, tq), 0)
            c = lax.broadcasted_iota(i32, (tq, tq), 1)
            keep = jnp.concatenate([c <= r, c <= r], axis=0)
            s = jnp.where(keep, s, -jnp.inf)
        m_old = m_ref[...]
        m_new = jnp.maximum(m_old, jnp.max(s, axis=-1, keepdims=True))
        corr = jnp.exp(m_old - m_new)
        p = jnp.exp(s - m_new)
        l_ref[...] = l_ref[...] * corr + jnp.sum(p, axis=-1, keepdims=True)
        acc_ref[...] = acc_ref[...] * corr + _dot(p.astype(bf16), vbf_ref[pl.ds(off, tq), :])
        m_ref[...] = m_new

    def body(j, carry):
        step(j, False)
        return carry

    lax.fori_loop(0, qi, body, 0)
    step(qi, True)

    o = acc_ref[...] / l_ref[...]
    lam = _lambda_scalar(lq1_ref[...], lk1_ref[...], lq2_ref[...], lk2_ref[...])
    o_ref[0] = _subln(o[:tq], o[tq:], lam, sw_ref[...]).astype(o_ref.dtype)


def _attn_prompt(qa, ka, va, prm):
    B, S, _ = qa.shape
    tq = min(S, 256)
    small = [prm["lq1"], prm["lk1"], prm["lq2"], prm["lk2"], prm["subln_w"]]
    return pl.pallas_call(
        functools.partial(_attn_prompt_kernel, tq=tq),
        grid=(B, DA_HEADS, S // tq),
        in_specs=[pl.BlockSpec((1, tq, 128), lambda b, h, i: (b, i, h)),
                  pl.BlockSpec((1, S, 128), lambda b, h, i: (b, 0, h)),
                  pl.BlockSpec((1, S, 128), lambda b, h, i: (b, 0, h))]
                 + [pl.BlockSpec(a.shape, lambda b, h, i: (0, 0)) for a in small],
        out_specs=pl.BlockSpec((1, tq, 128), lambda b, h, i: (b, i, h)),
        out_shape=jax.ShapeDtypeStruct((B, S, DA_HEADS * DA_V_DIM), bf16),
        scratch_shapes=[pltpu.VMEM((S, 128), bf16), pltpu.VMEM((S, 128), bf16),
                        pltpu.VMEM((2 * tq, 1), f32), pltpu.VMEM((2 * tq, 1), f32),
                        pltpu.VMEM((2 * tq, 128), f32)],
        compiler_params=_params("parallel", "parallel", "arbitrary"),
        name="attn_prompt",
    )(qa, ka, va, *small)


PAGES_PER_STEP = 8


def _attn_sample_kernel(pt_ref, q_ref, kn_ref, vn_ref, lq1_ref, lk1_ref, lq2_ref, lk2_ref, sw_ref, *rest):
    del pt_ref
    npg = PAGES_PER_STEP
    k_refs, v_refs = rest[:npg], rest[npg:2 * npg]
    o_ref, m_ref, l_ref, acc_ref = rest[2 * npg:]
    p_id = pl.program_id(1)
    n_hm = 2 * DA_HEADS

    @pl.when(p_id == 0)
    def _():
        m_ref[...] = jnp.full(m_ref.shape, -jnp.inf, f32)
        l_ref[...] = jnp.zeros(l_ref.shape, f32)
        acc_ref[...] = jnp.zeros(acc_ref.shape, f32)

    q = q_ref[0].astype(f32)
    col = lax.broadcasted_iota(i32, (n_hm, D_MODEL), 1)
    row = lax.broadcasted_iota(i32, (n_hm, D_MODEL), 0)
    qbd = jnp.where((col // DA_HEAD_DIM) == row, jnp.broadcast_to(q, (n_hm, D_MODEL)), 0.0).astype(bf16)

    def online(s, v_bf):
        m_old = m_ref[...]
        m_new = jnp.maximum(m_old, jnp.max(s, axis=-1, keepdims=True))
        corr = jnp.exp(m_old - m_new)
        p = jnp.exp(s - m_new)
        l_ref[...] = l_ref[...] * corr + jnp.sum(p, axis=-1, keepdims=True)
        acc_ref[...] = acc_ref[...] * corr + _dot(p.astype(bf16), v_bf)
        m_ref[...] = m_new

    for i in range(npg):
        s = _dot_nt(qbd, k_refs[i][0].astype(bf16))
        online(s, v_refs[i][0].astype(bf16))

    @pl.when(p_id == pl.num_programs(1) - 1)
    def _():
        kn = jnp.broadcast_to(kn_ref[0], (8, D_MODEL)).astype(bf16)
        vn = jnp.broadcast_to(vn_ref[0], (8, D_MODEL)).astype(bf16)
        s = _dot_nt(qbd, kn)
        s = jnp.where(lax.broadcasted_iota(i32, s.shape, 1) == 0, s, -jnp.inf)
        online(s, vn)
        o = acc_ref[...] / l_ref[...]
        lam = _lambda_scalar(lq1_ref[...], lk1_ref[...], lq2_ref[...], lk2_ref[...])
        for h in range(DA_HEADS):
            sl = slice(h * DA_V_DIM, (h + 1) * DA_V_DIM)
            y = _subln(o[2 * h:2 * h + 1, sl], o[2 * h + 1:2 * h + 2, sl], lam, sw_ref[...])
            o_ref[0, :, sl] = y.astype(o_ref.dtype)


def _attn_sample(qa, ka, va, cache_k, cache_v, page_table, prm):
    DB = qa.shape[0]
    n_pages = page_table.shape[1]
    npg = PAGES_PER_STEP
    ck = cache_k.reshape(cache_k.shape[1], PAGE_SIZE, D_MODEL)
    cv = cache_v.reshape(cache_v.shape[1], PAGE_SIZE, D_MODEL)
    small = [prm["lq1"], prm["lk1"], prm["lq2"], prm["lk2"], prm["subln_w"]]
    tok = pl.BlockSpec((1, 1, D_MODEL), lambda b, p, pt: (b, 0, 0))

    def page_spec(i):
        return pl.BlockSpec((1, PAGE_SIZE, D_MODEL), lambda b, p, pt: (pt[b, p * npg + i], 0, 0))

    grid_spec = pltpu.PrefetchScalarGridSpec(
        num_scalar_prefetch=1,
        grid=(DB, n_pages // npg),
        in_specs=[tok, tok, tok] + [pl.BlockSpec(a.shape, lambda b, p, pt: (0, 0)) for a in small]
                 + [page_spec(i) for i in range(npg)] * 2,
        out_specs=pl.BlockSpec((1, 1, D_MODEL), lambda b, p, pt: (b, 0, 0)),
        scratch_shapes=[pltpu.VMEM((2 * DA_HEADS, 1), f32), pltpu.VMEM((2 * DA_HEADS, 1), f32),
                        pltpu.VMEM((2 * DA_HEADS, D_MODEL), f32)],
    )
    return pl.pallas_call(
        _attn_sample_kernel,
        grid_spec=grid_spec,
        out_shape=jax.ShapeDtypeStruct((DB, 1, D_MODEL), bf16),
        compiler_params=_params("parallel", "arbitrary"),
        name="attn_sample",
    )(page_table, qa, ka, va, *small, *([ck] * npg), *([cv] * npg))


def _gla_out(o, r, gw):
    return _rms_rows(o, gw) * (r * jax.nn.sigmoid(r))


def _gla_prompt_kernel(q_ref, k_ref, v_ref, la_ref, r_ref, gw_ref, y_ref, s_ref, st_ref, *, n_chunks):
    C = GLA_CHUNK
    sb = pl.program_id(2)

    @pl.when(sb == 0)
    def _():
        st_ref[...] = jnp.zeros(st_ref.shape, f32)

    ri = lax.broadcasted_iota(i32, (C, C), 0)
    ci = lax.broadcasted_iota(i32, (C, C), 1)
    tri = (ci <= ri).astype(bf16)
    causal = ci <= ri

    def chunk(c, carry):
        rows = pl.ds(pl.multiple_of(c * C, C), C)
        q, k, v, g = q_ref[0, rows, :], k_ref[0, rows, :], v_ref[0, rows, :], la_ref[0, rows, :]
        g_hi, g_mid, g_lo = _split3(g)
        b = _dot(tri, g_hi) + _dot(tri, g_mid) + _dot(tri, g_lo)
        b_last = b[C - 1:C, :]
        q_dec = (q * jnp.exp(b)).astype(bf16)
        k_inv = (k * jnp.exp(-b)).astype(bf16)
        k_end = (k * jnp.exp(b_last - b)).astype(bf16)
        v_bf = v.astype(bf16)
        attn = jnp.where(causal, _dot_nt(q_dec, k_inv), 0.0)
        st = st_ref[...]
        o = _dot(attn.astype(bf16), v_bf) + _dot_nt(q_dec, st.astype(bf16))
        st_ref[...] = st * jnp.exp(b_last) + _dot_tn(v_bf, k_end)
        y_ref[0, rows, :] = _gla_out(o, r_ref[0, rows, :], gw_ref[...]).astype(y_ref.dtype)
        return carry

    lax.fori_loop(0, n_chunks, chunk, 0)

    @pl.when(sb == pl.num_programs(2) - 1)
    def _():
        s_ref[0, 0] = st_ref[...].T


def _gla_prompt(qb, kb, vb, la, r, prm):
    B, S, _ = qb.shape
    sblk = min(S, 512)
    kspec = pl.BlockSpec((1, sblk, GLA_DK), lambda b, h, s: (b, s, h))
    vspec = pl.BlockSpec((1, sblk, GLA_DV), lambda b, h, s: (b, s, h))
    return pl.pallas_call(
        functools.partial(_gla_prompt_kernel, n_chunks=sblk // GLA_CHUNK),
        grid=(B, GLA_HEADS, S // sblk),
        in_specs=[kspec, kspec, vspec, kspec, vspec, pl.BlockSpec((1, GLA_DV), lambda b, h, s: (0, 0))],
        out_specs=[vspec, pl.BlockSpec((1, 1, GLA_DK, GLA_DV), lambda b, h, s: (b, h, 0, 0))],
        out_shape=[jax.ShapeDtypeStruct((B, S, GLA_HEADS * GLA_DV), bf16),
                   jax.ShapeDtypeStruct((B, GLA_HEADS, GLA_DK, GLA_DV), f32)],
        scratch_shapes=[pltpu.VMEM((GLA_DV, GLA_DK), f32)],
        compiler_params=_params("parallel", "parallel", "arbitrary"),
        name="gla_prompt",
    )(qb, kb, vb, la, r, prm["gla_norm_w"])


def _gla_sample_kernel(q_ref, k_ref, v_ref, la_ref, r_ref, gw_ref, s0_ref, y_ref, s_ref):
    def column(row):
        return jnp.broadcast_to(row, (8, row.shape[1])).T[:, 0:1]

    for h in range(GLA_HEADS):
        ks = slice(h * GLA_DK, (h + 1) * GLA_DK)
        vs = slice(h * GLA_DV, (h + 1) * GLA_DV)
        q, k, v, g = q_ref[0, :, ks], k_ref[0, :, ks], v_ref[0, :, vs], la_ref[0, :, ks]
        s0 = s0_ref[0, h]
        a = jnp.exp(g)
        qd = q * a
        o = jnp.sum(column(qd) * s0, axis=0, keepdims=True) + jnp.sum(qd * (k * jnp.exp(-g)), keepdims=True) * v
        s_ref[0, h] = column(a) * s0 + column(k) * v
        y_ref[0, :, vs] = _gla_out(o, r_ref[0, :, vs], gw_ref[...]).astype(y_ref.dtype)


def _gla_sample(qb, kb, vb, la, r, s0, prm):
    DB = qb.shape[0]
    tok = lambda w: pl.BlockSpec((1, 1, w), lambda b: (b, 0, 0))
    st = pl.BlockSpec((1, GLA_HEADS, GLA_DK, GLA_DV), lambda b: (b, 0, 0, 0))
    return pl.pallas_call(
        _gla_sample_kernel,
        grid=(DB,),
        in_specs=[tok(512), tok(512), tok(1024), tok(512), tok(1024),
                  pl.BlockSpec((1, GLA_DV), lambda b: (0, 0)), st],
        out_specs=[tok(1024), st],
        out_shape=[jax.ShapeDtypeStruct((DB, 1, GLA_HEADS * GLA_DV), bf16),
                   jax.ShapeDtypeStruct(s0.shape, f32)],
        compiler_params=_params("parallel"),
        name="gla_sample",
    )(qb, kb, vb, la, r, prm["gla_norm_w"], s0)


def _merge_kernel(x_ref, ya_ref, yb_ref, gt_ref, wa_ref, wb_ref, wo_ref, o_ref):
    ga = jax.nn.sigmoid(gt_ref[:, :D_MODEL])
    gb = jax.nn.sigmoid(gt_ref[:, D_MODEL:])
    mixed = ga * _dot(ya_ref[...], wa_ref[...]) + gb * _dot(yb_ref[...], wb_ref[...])
    o_ref[...] = x_ref[...] + _dot(mixed.astype(bf16), wo_ref[...])


def _merge(x2, ya, yb, gt, prm):
    T = x2.shape[0]
    tm = min(T, 512)
    row = lambda w: pl.BlockSpec((tm, w), lambda i: (i, 0))
    ws = [prm["w_br_a"], prm["w_br_b"], prm["w_out"]]
    return pl.pallas_call(
        _merge_kernel,
        grid=(T // tm,),
        in_specs=[row(D_MODEL), row(D_MODEL), row(D_MODEL), row(2 * D_MODEL)] + [_resident(w.shape) for w in ws],
        out_specs=row(D_MODEL),
        out_shape=jax.ShapeDtypeStruct((T, D_MODEL), f32),
        compiler_params=_params("parallel"),
        name="merge",
    )(x2, ya, yb, gt, *ws)


def _topk_rows(s, k_top, v_ref, i_ref):
    n = s.shape[0]
    io = lax.broadcasted_iota(i32, s.shape, 0)
    for kk in range(k_top):
        m = jnp.max(s, axis=0, keepdims=True)
        idx = jnp.min(jnp.where(s == m, io, n), axis=0, keepdims=True)
        v_ref[kk:kk + 1, :] = m
        i_ref[kk:kk + 1, :] = idx
        s = jnp.where(io == idx, -jnp.inf, s)


def _route_kernel(x_ref, n2_ref, wq_ref, k1_ref, k2_ref, hn_ref, ridx_ref, cidx_ref, g_ref,
                  q_scr, v1_scr, i1_scr, v2_scr, i2_scr, cand_scr, vs_scr, ic_scr, rT_scr, cT_scr, gT_scr):
    K = PEER_TOPK
    hn = _rms_rows(x_ref[...], n2_ref[...]).astype(bf16)
    hn_ref[...] = hn
    for hm in range(2 * PEER_HEADS):
        q_scr[hm] = _dot(hn, wq_ref[:, hm * PEER_HALF:(hm + 1) * PEER_HALF])
    k1h, k1l = _split2(k1_ref[...])
    k2h, k2l = _split2(k2_ref[...])
    io16 = lax.broadcasted_iota(i32, (K, x_ref.shape[0]), 0)

    def scores(kh, kl, q):
        qh, ql = _split2(q)
        return _dot_nt(kh, qh) + _dot_nt(kl, qh) + _dot_nt(kh, ql)

    def head(h, carry):
        _topk_rows(scores(k1h, k1l, q_scr[2 * h]), K, v1_scr, i1_scr)
        _topk_rows(scores(k2h, k2l, q_scr[2 * h + 1]), K, v2_scr, i2_scr)
        v2 = v2_scr[...]
        for a in range(K):
            cand_scr[a * K:(a + 1) * K, :] = v1_scr[a:a + 1, :] + v2
        _topk_rows(cand_scr[...], K, vs_scr, ic_scr)
        ic = ic_scr[...]
        i1, i2 = i1_scr[...], i2_scr[...]
        base = pl.multiple_of(h * K, K)
        for kk in range(K):
            ia = ic[kk:kk + 1, :] // K
            ib = ic[kk:kk + 1, :] % K
            rT_scr[pl.ds(base + kk, 1), :] = jnp.sum(jnp.where(io16 == ia, i1, 0), axis=0, keepdims=True)
            cT_scr[pl.ds(base + kk, 1), :] = jnp.sum(jnp.where(io16 == ib, i2, 0), axis=0, keepdims=True)
        vs = vs_scr[...]
        e = jnp.exp(vs - vs[0:1, :])
        gT_scr[pl.ds(base, K), :] = e / jnp.sum(e, axis=0, keepdims=True)
        return carry

    lax.fori_loop(0, PEER_HEADS, head, 0)
    ridx_ref[...] = rT_scr[...].T
    cidx_ref[...] = cT_scr[...].T
    g_ref[...] = gT_scr[...].T


def _route(x1, prm):
    T = x1.shape[0]
    tb = min(T, 128)
    K = PEER_TOPK
    NS = PEER_HEADS * K
    row = lambda w: pl.BlockSpec((tb, w), lambda i: (i, 0))
    ins = [x1, prm["norm2_w"], prm["w_q_peer"], prm["sub_keys1"], prm["sub_keys2"]]
    return pl.pallas_call(
        _route_kernel,
        grid=(T // tb,),
        in_specs=[row(D_MODEL)] + [_resident(a.shape) for a in ins[1:]],
        out_specs=[row(D_MODEL), row(NS), row(NS), row(NS)],
        out_shape=[jax.ShapeDtypeStruct((T, D_MODEL), bf16), jax.ShapeDtypeStruct((T, NS), i32),
                   jax.ShapeDtypeStruct((T, NS), i32), jax.ShapeDtypeStruct((T, NS), f32)],
        scratch_shapes=[pltpu.VMEM((2 * PEER_HEADS, tb, PEER_HALF), f32),
                        pltpu.VMEM((K, tb), f32), pltpu.VMEM((K, tb), i32),
                        pltpu.VMEM((K, tb), f32), pltpu.VMEM((K, tb), i32),
                        pltpu.VMEM((K * K, tb), f32), pltpu.VMEM((K, tb), f32), pltpu.VMEM((K, tb), i32),
                        pltpu.VMEM((NS, tb), i32), pltpu.VMEM((NS, tb), i32), pltpu.VMEM((NS, tb), f32)],
        compiler_params=_params("parallel"),
        name="peer_route",
    )(*ins)


def _peer_act_kernel(hn_ref, u_ref, ridx_ref, cidx_ref, g_ref, w_ref, acc_ref, *, rows_per_step):
    j = pl.program_id(1)

    @pl.when(j == 0)
    def _():
        acc_ref[...] = jnp.zeros(acc_ref.shape, f32)

    a = _dot_nt(hn_ref[...], u_ref[...])
    ridx, cidx = ridx_ref[...], cidx_ref[...]
    acc = acc_ref[...]
    for rl in range(rows_per_step):
        picked = jnp.take_along_axis(a[:, rl * N_KEYS:(rl + 1) * N_KEYS], cidx, axis=1)
        acc = jnp.where(ridx == j * rows_per_step + rl, picked, acc)
    acc_ref[...] = acc

    @pl.when(j == pl.num_programs(1) - 1)
    def _():
        act = acc_ref[...]
        gelu = 0.5 * act * (1.0 + lax.erf(act * (2.0 ** -0.5)))
        w_ref[...] = g_ref[...] * gelu


def _peer_act(hn, u_bf, ridx, cidx, g):
    T = hn.shape[0]
    NE = u_bf.shape[0]
    tb = min(T, 256)
    eb = 1024
    NS = ridx.shape[1]
    tok = lambda w: pl.BlockSpec((tb, w), lambda i, j: (i, 0))
    return pl.pallas_call(
        functools.partial(_peer_act_kernel, rows_per_step=eb // N_KEYS),
        grid=(T // tb, NE // eb),
        in_specs=[tok(D_MODEL), pl.BlockSpec((eb, D_MODEL), lambda i, j: (j, 0)), tok(NS), tok(NS), tok(NS)],
        out_specs=tok(NS),
        out_shape=jax.ShapeDtypeStruct((T, NS), f32),
        scratch_shapes=[pltpu.VMEM((tb, NS), f32)],
        compiler_params=_params("parallel", "arbitrary"),
        name="peer_act",
    )(hn, u_bf, ridx, cidx, g)


def _peer_out_kernel(x_ref, w_ref, ridx_ref, cidx_ref, v_ref, o_ref, w3_ref, *, rows_per_step):
    j = pl.program_id(1)
    tb = x_ref.shape[0]

    @pl.when(j == 0)
    def _():
        o_ref[...] = x_ref[...]
        sub = lax.broadcasted_iota(i32, (N_KEYS, w_ref.shape[1]), 0)

        def scatter_token(t, carry):
            row = pl.ds(t, 1)
            r_hot = jnp.where(sub == ridx_ref[row, :], w_ref[row, :], 0.0).astype(bf16)
            c_hot = jnp.where(sub == cidx_ref[row, :], 1.0, 0.0).astype(bf16)
            w3_ref[pl.ds(pl.multiple_of(t * N_KEYS, N_KEYS), N_KEYS), :] = _dot_nt(r_hot, c_hot)
            return carry

        lax.fori_loop(0, tb, scatter_token, 0)

    for rl in range(0, rows_per_step, 2):
        r0 = j * rows_per_step + rl
        lhs = jnp.concatenate([w3_ref[pl.ds(r0, tb, stride=N_KEYS), :],
                               w3_ref[pl.ds(r0 + 1, tb, stride=N_KEYS), :]], axis=1).astype(bf16)
        o_ref[...] += _dot(lhs, v_ref[rl * N_KEYS:(rl + 2) * N_KEYS, :])


def _peer_out(x1, w, ridx, cidx, v_bf):
    T = x1.shape[0]
    NE = v_bf.shape[0]
    tb = min(T, 256)
    eb = 1024
    NS = ridx.shape[1]
    tok = lambda wd: pl.BlockSpec((tb, wd), lambda i, j: (i, 0))
    return pl.pallas_call(
        functools.partial(_peer_out_kernel, rows_per_step=eb // N_KEYS),
        grid=(T // tb, NE // eb),
        in_specs=[tok(D_MODEL), tok(NS), tok(NS), tok(NS), pl.BlockSpec((eb, D_MODEL), lambda i, j: (j, 0))],
        out_specs=tok(D_MODEL),
        out_shape=jax.ShapeDtypeStruct((T, D_MODEL), f32),
        scratch_shapes=[pltpu.VMEM((tb * N_KEYS, N_KEYS), f32)],
        compiler_params=_params("parallel", "arbitrary"),
        name="peer_out",
    )(x1, w, ridx, cidx, v_bf)


def _prepare(norm1_w, w_in, q_norm_w, k_norm_w, lambda_q1, lambda_k1, lambda_q2, lambda_k2, subln_w, w_gla_g2,
             b_gla_g, gla_norm_w, w_br_a, w_br_b, w_out, norm2_w, w_q_peer, sub_keys1, sub_keys2, expert_u,
             expert_v, layer):
    l = layer
    w = w_in[l]
    off = [0, 1024, 2048, 3072, 3584, 4096, 5120, 5136, 6160, 8208]
    g2 = jnp.pad(w_gla_g2[l], ((0, LANES - GLA_GATE_RANK), (0, 0)))
    g2h = g2.astype(bf16)
    blk = jnp.arange(256) // DA_HEAD_DIM
    row = lambda v: v.reshape(1, -1).astype(f32)
    return dict(
        norm1_w=row(norm1_w[l]),
        w_da=w[:, off[0]:off[3]].astype(bf16),
        w_gla=w[:, off[3]:off[6]].astype(bf16),
        w_g=jnp.pad(w[:, off[6]:off[7]], ((0, 0), (0, LANES - GLA_GATE_RANK))).astype(bf16),
        w_r=w[:, off[7]:off[8]].astype(bf16),
        w_gate=w[:, off[8]:off[9]].astype(bf16),
        qn=row(jnp.tile(q_norm_w[l], 4)), kn=row(jnp.tile(k_norm_w[l], 4)),
        g2h=g2h, g2l=(g2 - g2h.astype(f32)).astype(bf16), bg=row(b_gla_g[l]),
        gsum=(blk[:, None] == blk[None, :]).astype(bf16),
        lq1=row(lambda_q1[l]), lk1=row(lambda_k1[l]), lq2=row(lambda_q2[l]), lk2=row(lambda_k2[l]),
        subln_w=row(subln_w[l]), gla_norm_w=row(gla_norm_w[l]),
        w_br_a=w_br_a[l].astype(bf16), w_br_b=w_br_b[l].astype(bf16), w_out=w_out[l].astype(bf16),
        norm2_w=row(norm2_w[l]), w_q_peer=w_q_peer[l].astype(bf16),
        sub_keys1=sub_keys1[l].astype(f32), sub_keys2=sub_keys2[l].astype(f32),
        expert_u=expert_u[l].astype(bf16), expert_v=expert_v[l].astype(bf16),
    )


def _layer(x, prm, mixers):
    B, S, _ = x.shape
    x2 = x.reshape(B * S, D_MODEL)
    qa, ka, va, qb, kb, vb, la, r, gt = _inproj(x2, prm)
    sh = lambda a: a.reshape(B, S, a.shape[-1])
    y_a, y_b, state = mixers(sh(qa), sh(ka), sh(va), sh(qb), sh(kb), sh(vb), sh(la), sh(r))
    x1 = _merge(x2, y_a.reshape(B * S, -1), y_b.reshape(B * S, -1), gt, prm)
    hn, ridx, cidx, g = _route(x1, prm)
    w = _peer_act(hn, prm["expert_u"], ridx, cidx, g)
    y = _peer_out(x1, w, ridx, cidx, prm["expert_v"])
    k_new = ka.reshape(B, S, DA_HEADS, 2 * DA_HEAD_DIM)
    v_new = va.reshape(B, S, DA_HEADS, DA_V_DIM)
    return y.reshape(B, S, D_MODEL), k_new, v_new, state


def kernel(x_prompt, x_sample, cache_k, cache_v, state_gla, page_table, norm1_w, w_in, q_norm_w, k_norm_w, lambda_q1, lambda_k1, lambda_q2, lambda_k2, subln_w, w_gla_g2, b_gla_g, gla_norm_w, w_br_a, w_br_b, w_out, norm2_w, w_q_peer, sub_keys1, sub_keys2, expert_u, expert_v):
    depth = w_in.shape[0]
    assert depth == 1, "single trunk layer"
    prm = _prepare(norm1_w, w_in, q_norm_w, k_norm_w, lambda_q1, lambda_k1, lambda_q2, lambda_k2, subln_w,
                   w_gla_g2, b_gla_g, gla_norm_w, w_br_a, w_br_b, w_out, norm2_w, w_q_peer, sub_keys1, sub_keys2,
                   expert_u, expert_v, 0)

    def prompt_mixers(qa, ka, va, qb, kb, vb, la, r):
        y_a = _attn_prompt(qa, ka, va, prm)
        y_b, s_fin = _gla_prompt(qb, kb, vb, la, r, prm)
        return y_a, y_b, s_fin

    def sample_mixers(qa, ka, va, qb, kb, vb, la, r):
        y_a = _attn_sample(qa, ka, va, cache_k, cache_v, page_table, prm)
        y_b, s_fin = _gla_sample(qb, kb, vb, la, r, state_gla[0], prm)
        return y_a, y_b, s_fin

    y_p, k_p, v_p, s_p = _layer(x_prompt, prm, prompt_mixers)
    y_s, k_s, v_s, s_s = _layer(x_sample, prm, sample_mixers)
    return (y_p, y_s, k_p[None], v_p[None], s_p[None], k_s[None], v_s[None], s_s[None])
```
